```python
import math
import jax
import jax.numpy as jnp
from jax import lax
import numpy as np

D_MODEL = 2048
BATCH = 4
SEQ = 4096
DEPTH = 4

GRID_W = 64
CTX_LEN = 256
SSD_HEADS = 16
SSD_HEAD_DIM = 64
SSD_GROUPS = 2
SSD_STATE = 128
SSD_CONV = 4
SSD_CHUNK = 128
SSD_INNER = SSD_HEADS * SSD_HEAD_DIM
SSD_XBC = SSD_INNER + 2 * SSD_GROUPS * SSD_STATE
MLA_HEADS = 8
MLA_Q_RANK = 512
MLA_KV_RANK = 512
MLA_NOPE = 128
MLA_ROPE = 64
MLA_V = 128
MLA_SCALE = (MLA_NOPE + MLA_ROPE) ** -0.5
ATTN_BLOCK = 128
ROPE_BASE = 10000.0
MLSTM_HEADS = 8
MLSTM_QK = 128
MLSTM_V = 256
MLSTM_CONV = 4
MLSTM_CHUNK = 128
N_EXPERTS = 64
TOP_K = 8
EXPERT_HIDDEN = 384
ROUTE_SCALE = 2.5
MOE_BLOCK = 128
ALPHA = (2 * DEPTH) ** 0.25
BETA = (8 * DEPTH) ** -0.25
N_EVEN = (DEPTH + 1) // 2
N_ODD = DEPTH // 2
EVEN_SIZES = (SSD_INNER, SSD_XBC, SSD_HEADS, SSD_HEADS, MLA_Q_RANK, MLA_KV_RANK, MLA_ROPE)
EVEN_WIDTH = sum(EVEN_SIZES)
EVEN_OUT = SSD_INNER + MLA_HEADS * MLA_V
ODD_SIZES = (2 * MLSTM_HEADS * MLSTM_QK, MLSTM_HEADS * MLSTM_V, MLSTM_HEADS * MLSTM_V, 4 * MLSTM_HEADS)
ODD_WIDTH = sum(ODD_SIZES)
ODD_OUT = MLSTM_HEADS * MLSTM_V
EPS = 1e-6

kernel_name = 'hybrid_ssd_mla_mlstm_moe_flow_backbone'


def rms_norm(h, g):
    hf = h.astype(jnp.float32)
    y = hf * lax.rsqrt(jnp.mean(hf * hf, axis=-1, keepdims=True) + EPS) * g.astype(jnp.float32)
    return y.astype(h.dtype)


def layer_norm(h, g, b):
    hf = h.astype(jnp.float32)
    mu = jnp.mean(hf, axis=-1, keepdims=True)
    var = jnp.mean(jnp.square(hf - mu), axis=-1, keepdims=True)
    return ((hf - mu) * lax.rsqrt(var + 1e-5) * g.astype(jnp.float32) + b.astype(jnp.float32)).astype(h.dtype)


def modulate(h, shift, scale):
    return h * (1 + scale) + shift


def split_cols(p, sizes):
    idx, acc = [], 0
    for s in sizes[:-1]:
        acc += s
        idx.append(acc)
    return jnp.split(p, idx, axis=-1)


def rev(t, axis):
    return jnp.flip(t, axis=axis)


def dwconv(h, w, bias):
    k = w.shape[0]
    y = lax.conv_general_dilated(h, w[:, None, :].astype(h.dtype), window_strides=(1,),
                                 padding=[((k - 1) // 2, k // 2)],
                                 dimension_numbers=('NWC', 'WIO', 'NWC'),
                                 feature_group_count=h.shape[-1])
    return y + bias


def axial_rope(n_rows):
    n_freq = MLA_ROPE // 4
    inv = ROPE_BASE ** (-jnp.arange(n_freq, dtype=jnp.float32) / n_freq)
    rows = jnp.arange(n_rows, dtype=jnp.float32)
    cols = jnp.arange(GRID_W, dtype=jnp.float32)
    ang_r = jnp.broadcast_to(rows[:, None, None] * inv, (n_rows, GRID_W, n_freq))
    ang_c = jnp.broadcast_to(cols[None, :, None] * inv, (n_rows, GRID_W, n_freq))
    ang = jnp.stack([ang_r, ang_c], axis=2).reshape(n_rows * GRID_W, 2, n_freq)
    return jnp.cos(ang), jnp.sin(ang)


def apply_rope(t, cos, sin):
    tr = t.reshape(t.shape[:-1] + (2, 2, MLA_ROPE // 4)).astype(jnp.float32)
    t1, t2 = tr[..., 0, :], tr[..., 1, :]
    out = jnp.stack([t1 * cos - t2 * sin, t2 * cos + t1 * sin], axis=-2)
    return out.reshape(t.shape).astype(t.dtype)


def segsum(a):
    t = a.shape[-1]
    a_rep = jnp.broadcast_to(a[..., :, None], a.shape + (t,))
    strict = jnp.tril(jnp.ones((t, t), bool), -1)
    cs = jnp.cumsum(jnp.where(strict, a_rep, 0.0), axis=-2)
    return jnp.where(jnp.tril(jnp.ones((t, t), bool)), cs, -jnp.inf)


def ssd_chunked(x, a, bm, cm, h0):
    b, L, G, E, P = x.shape
    q = SSD_CHUNK
    nc = L // q
    x = x.reshape(b, nc, q, G, E, P)
    bm = bm.reshape(b, nc, q, G, -1)
    cm = cm.reshape(b, nc, q, G, -1)
    a = jnp.moveaxis(a.reshape(b, nc, q, G, E), (3, 4), (1, 2))
    a_cum = jnp.cumsum(a, axis=-1)
    lmat = jnp.exp(segsum(a))
    cb = jnp.einsum('bclgn,bcsgn->bgcls', cm, bm)
    y_diag = jnp.einsum('bgecls,bcsgep->bclgep', cb[:, :, None] * lmat, x)
    decay_in = jnp.moveaxis(jnp.exp(a_cum[..., -1:] - a_cum), (1, 2), (3, 4))
    states = jnp.einsum('bclgn,bclgep->bcgepn', bm, x * decay_in[..., None])
    states = jnp.concatenate([h0[:, None], states], axis=1)
    chunk_a = jnp.pad(a_cum[..., -1], ((0, 0), (0, 0), (0, 0), (1, 0)))
    chunk_decay = jnp.exp(segsum(chunk_a))
    states = jnp.einsum('bgezc,bcgepn->bzgepn', chunk_decay, states)
    y_off = jnp.einsum('bclgn,bcgepn,bgecl->bclgep', cm, states[:, :-1], jnp.exp(a_cum))
    y = (y_diag + y_off).reshape(b, L, G, E, P)
    return y, states[:, -1]


def ssd_direction(xs, dt_raw, bm, cm, a_log, dt_bias, d_skip, h0):
    b, L, _ = xs.shape
    e = SSD_HEADS // SSD_GROUPS
    dt = jax.nn.softplus(dt_raw.astype(jnp.float32) + dt_bias.astype(jnp.float32)).reshape(b, L, SSD_GROUPS, e)
    xh = xs.astype(jnp.float32).reshape(b, L, SSD_GROUPS, e, SSD_HEAD_DIM)
    a_cont = -jnp.exp(a_log.astype(jnp.float32)).reshape(SSD_GROUPS, e)
    y, h_fin = ssd_chunked(xh * dt[..., None], dt * a_cont, bm.astype(jnp.float32), cm.astype(jnp.float32), h0)
    y = y + d_skip.astype(jnp.float32).reshape(SSD_GROUPS, e)[..., None] * xh
    return y.reshape(b, L, SSD_INNER), h_fin


def ssd_bidir(xs, dt_f, dt_b, bm, cm, a_log, dt_bias, d_skip, h0_f, h0_b):
    y_f, h_f = ssd_direction(xs, dt_f, bm, cm, a_log[0], dt_bias[0], d_skip[0], h0_f)
    y_b, h_b = ssd_direction(rev(xs, 1), rev(dt_b, 1), rev(bm, 1), rev(cm, 1), a_log[1], dt_bias[1], d_skip[1], h0_b)
    return y_f + rev(y_b, 1), h_f, h_b


def mla_heads(c_q, c_kv, k_r, q_norm, w_uq, kv_norm, w_ukv, rope):
    b, L, _ = c_q.shape
    qh = (rms_norm(c_q, q_norm) @ w_uq).reshape(b, L, MLA_HEADS, MLA_NOPE + MLA_ROPE)
    qh = jnp.moveaxis(qh, 2, 1) * MLA_SCALE
    kv = jnp.moveaxis((rms_norm(c_kv, kv_norm) @ w_ukv).reshape(b, L, MLA_HEADS, MLA_NOPE + MLA_V), 2, 1)
    q_nope, q_rope = qh[..., :MLA_NOPE], qh[..., MLA_NOPE:]
    k_nope, v = kv[..., :MLA_NOPE], kv[..., MLA_NOPE:]
    if rope is not None:
        cos, sin = rope
        q_rope = apply_rope(q_rope, cos, sin)
        k_r = apply_rope(k_r, cos, sin)
    return q_nope, q_rope, k_nope, k_r, v


def block_attention(q_nope, q_rope, k_nope, k_rope, v):
    b, h, L, _ = q_nope.shape
    nb = L // ATTN_BLOCK

    def to_blocks(t):
        return jnp.moveaxis(t.reshape(b, h, nb, ATTN_BLOCK, t.shape[-1]), 2, 0)

    def one_block(qs):
        qn, qr = qs
        s = jnp.einsum('bhqd,bhkd->bhqk', qn, k_nope) + jnp.einsum('bhqr,bkr->bhqk', qr, k_rope)
        p = jax.nn.softmax(s.astype(jnp.float32), axis=-1)
        return jnp.einsum('bhqk,bhkd->bhqd', p.astype(v.dtype), v)

    o = lax.map(one_block, (to_blocks(q_nope), to_blocks(q_rope)))
    return jnp.moveaxis(o, 0, 2).reshape(b, h, L, v.shape[-1])


def ssd_mla_mixer(u_lat, u_ctx, w_in, conv_w, conv_b, a_log, dt_bias, d_skip, ssd_g,
                  q_norm, w_uq, kv_norm, w_ukv, w_out, cos, sin):
    def project(u, rope):
        z, xbc, dt_f, dt_b, c_q, c_kv, k_r = split_cols(u @ w_in, EVEN_SIZES)
        xbc = jax.nn.silu(dwconv(xbc, conv_w, conv_b))
        xs, bm, cm = split_cols(xbc, (SSD_INNER, SSD_GROUPS * SSD_STATE, SSD_GROUPS * SSD_STATE))
        bl = u.shape[:2]
        bm = bm.reshape(bl + (SSD_GROUPS, SSD_STATE))
        cm = cm.reshape(bl + (SSD_GROUPS, SSD_STATE))
        return z, (xs, dt_f, dt_b, bm, cm), mla_heads(c_q, c_kv, k_r, q_norm, w_uq, kv_norm, w_ukv, rope)

    z_c, ssd_c, (qn_c, qr_c, kn_c, kr_c, v_c) = project(u_ctx, None)
    z_l, ssd_l, (qn_l, qr_l, kn_l, kr_l, v_l) = project(u_lat, (cos, sin))
    b = u_ctx.shape[0]
    h_zero = jnp.zeros((b, SSD_GROUPS, SSD_HEADS // SSD_GROUPS, SSD_HEAD_DIM, SSD_STATE), jnp.float32)
    y_c, hs_f, hs_b = ssd_bidir(*ssd_c, a_log, dt_bias, d_skip, h_zero, h_zero)
    y_l, _, _ = ssd_bidir(*ssd_l, a_log, dt_bias, d_skip, hs_f, hs_b)
    o_c = block_attention(qn_c, qr_c, kn_c, kr_c, v_c)
    o_l = block_attention(qn_l, qr_l, jnp.concatenate([kn_c, kn_l], axis=2),
                          jnp.concatenate([kr_c, kr_l], axis=1), jnp.concatenate([v_c, v_l], axis=2))

    def merge(y, z, o):
        y = rms_norm(y.astype(z.dtype) * jax.nn.silu(z), ssd_g)
        bb, _, L, _ = o.shape
        o = jnp.moveaxis(o, 1, 2).reshape(bb, L, MLA_HEADS * MLA_V)
        return jnp.concatenate([y, o.astype(y.dtype)], axis=-1) @ w_out

    return merge(y_l, z_l, o_l), merge(y_c, z_c, o_c)


def mlstm_chunked(q, k, v, ig, lf, state0):
    b, h, L, dk = q.shape
    dv = v.shape[-1]
    qc = MLSTM_CHUNK
    nc = L // qc
    q = q.reshape(b, h, nc, qc, dk)
    k = k.reshape(b, h, nc, qc, dk)
    v = v.reshape(b, h, nc, qc, dv)
    ig = ig.reshape(b, h, nc, qc)
    bcum = jnp.cumsum(lf.reshape(b, h, nc, qc), axis=-1)
    causal = jnp.tril(jnp.ones((qc, qc), bool))
    log_w = jnp.where(causal, bcum[..., :, None] - bcum[..., None, :] + ig[..., None, :], -jnp.inf)
    g = bcum[..., -1:] - bcum + ig
    g_max = jnp.max(g, axis=-1)
    wk = jnp.exp(g - g_max[..., None])[..., None] * k
    c_loc = jnp.einsum('bhcsv,bhcsk->bhcvk', v, wk)
    n_loc = jnp.sum(wk, axis=-2)

    def step(carry, inp):
        c_st, n_st, m_st = carry
        cl, nl, gm, bl = inp
        m_new = jnp.maximum(bl + m_st, gm)
        dec = jnp.exp(bl + m_st - m_new)
        sc = jnp.exp(gm - m_new)
        c_new = dec[..., None, None] * c_st + sc[..., None, None] * cl
        n_new = dec[..., None] * n_st + sc[..., None] * nl
        return (c_new, n_new, m_new), (c_st, n_st, m_st)

    xs = (jnp.moveaxis(c_loc, 2, 0), jnp.moveaxis(n_loc, 2, 0), jnp.moveaxis(g_max, 2, 0),
          jnp.moveaxis(bcum[..., -1], 2, 0))
    final, (c_in, n_in, m_in) = lax.scan(step, state0, xs)
    c_in = jnp.moveaxis(c_in, 0, 2)
    n_in = jnp.moveaxis(n_in, 0, 2)
    m_in = jnp.moveaxis(m_in, 0, 2)
    m_inter = bcum + m_in[..., None]
    m_t = jnp.maximum(m_inter, jnp.max(log_w, axis=-1))
    s = jnp.einsum('bhctk,bhcsk->bhcts', q, k) * jnp.exp(log_w - m_t[..., None])
    w_inter = jnp.exp(m_inter - m_t)
    num = jnp.einsum('bhcts,bhcsv->bhctv', s, v) + w_inter[..., None] * jnp.einsum('bhctk,bhcvk->bhctv', q, c_in)
    den = jnp.sum(s, axis=-1) + w_inter * jnp.einsum('bhctk,bhck->bhct', q, n_in)
    out = num / jnp.maximum(jnp.abs(den), jnp.exp(-m_t))[..., None]
    return out.reshape(b, h, L, dv), final


def mlstm_bidir(q, k, v, ig_f, lf_f, ig_b, lf_b, state_f, state_b):
    h_f, s_f = mlstm_chunked(q, k, v, ig_f, lf_f, state_f)
    h_b, s_b = mlstm_chunked(rev(q, 2), rev(k, 2), rev(v, 2), rev(ig_b, 2), rev(lf_b, 2), state_b)
    return h_f + rev(h_b, 2), s_f, s_b


def mlstm_mixer(u_lat, u_ctx, w_in, gate_b, conv_w, conv_b, norm_g, w_out):
    def project(u):
        b, L, _ = u.shape
        qk, v, o, gates = split_cols(u @ w_in, ODD_SIZES)
        qk = jax.nn.silu(dwconv(qk, conv_w, conv_b)).astype(jnp.float32)
        q, k = jnp.split(qk, 2, axis=-1)

        def heads(t, d):
            return jnp.moveaxis(t.reshape(b, L, MLSTM_HEADS, d), 2, 1)

        q = heads(q, MLSTM_QK)
        k = heads(k, MLSTM_QK) * MLSTM_QK ** -0.5
        v = heads(v.astype(jnp.float32), MLSTM_V)
        g = gates.reshape(b, L, 4, MLSTM_HEADS).astype(jnp.float32) + gate_b.astype(jnp.float32)
        g = jnp.moveaxis(g, (2, 3), (0, 2))
        return (q, k, v, g[0], jax.nn.log_sigmoid(g[1]), g[2], jax.nn.log_sigmoid(g[3])), o

    in_c, o_c = project(u_ctx)
    in_l, o_l = project(u_lat)
    b = u_ctx.shape[0]
    zero = (jnp.zeros((b, MLSTM_HEADS, MLSTM_V, MLSTM_QK), jnp.float32),
            jnp.zeros((b, MLSTM_HEADS, MLSTM_QK), jnp.float32),
            jnp.zeros((b, MLSTM_HEADS), jnp.float32))
    h_c, s_f, s_b = mlstm_bidir(*in_c, zero, zero)
    h_l, _, _ = mlstm_bidir(*in_l, s_f, s_b)

    def merge(hh, o):
        bb, _, L, _ = hh.shape
        mu = jnp.mean(hh, axis=-1, keepdims=True)
        var = jnp.mean(jnp.square(hh - mu), axis=-1, keepdims=True)
        hn = jnp.moveaxis((hh - mu) * lax.rsqrt(var + EPS), 1, 2).reshape(bb, L, ODD_OUT)
        hn = hn * norm_g.astype(jnp.float32)
        return (hn.astype(o.dtype) * jax.nn.sigmoid(o)) @ w_out

    return merge(h_l, o_l), merge(h_c, o_c)


def swiglu(h, w_gu, w_down):
    gate, up = jnp.split(h @ w_gu, 2, axis=-1)
    return (jax.nn.silu(gate) * up) @ w_down


def moe_ffn(h, router_w, router_bias, w_gu, w_down, sh_gu, sh_down):
    t, d = h.shape
    scores = jax.nn.sigmoid((h @ router_w).astype(jnp.float32))
    _, idx = lax.top_k(scores + router_bias.astype(jnp.float32), TOP_K)
    gates = jnp.take_along_axis(scores, idx, axis=-1)
    gates = ROUTE_SCALE * gates / jnp.sum(gates, axis=-1, keepdims=True)
    flat_e = idx.reshape(-1)
    order = jnp.argsort(flat_e)
    sorted_e = flat_e[order]
    counts = jnp.bincount(flat_e, length=N_EXPERTS)
    padded = (counts + MOE_BLOCK - 1) // MOE_BLOCK * MOE_BLOCK
    ends = jnp.cumsum(counts)
    pends = jnp.cumsum(padded)
    dest = (pends - padded)[sorted_e] + jnp.arange(t * TOP_K) - (ends - counts)[sorted_e]
    n_blocks = (t * TOP_K + N_EXPERTS * (MOE_BLOCK - 1) + MOE_BLOCK - 1) // MOE_BLOCK
    n_rows = n_blocks * MOE_BLOCK
    row_tok = jnp.full((n_rows,), t, jnp.int32).at[dest].set((order // TOP_K).astype(jnp.int32))
    row_w = jnp.zeros((n_rows,), jnp.float32).at[dest].set(gates.reshape(-1)[order])
    blk_e = jnp.minimum(jnp.searchsorted(pends, jnp.arange(n_blocks) * MOE_BLOCK, side='right'), N_EXPERTS - 1)
    h_pad = jnp.concatenate([h, jnp.zeros((1, d), h.dtype)], axis=0)

    def body(acc, blk):
        tok, w, e = blk
        y = swiglu(h_pad[tok], w_gu[e], w_down[e])
        return acc.at[tok].add(y * w[:, None].astype(y.dtype)), None

    acc, _ = lax.scan(body, jnp.zeros((t + 1, d), h.dtype),
                      (row_tok.reshape(n_blocks, MOE_BLOCK), row_w.reshape(n_blocks, MOE_BLOCK), blk_e))
    return acc[:t] + swiglu(h, sh_gu, sh_down)


def setup_inputs(seed: int = 0) -> dict:
    key = jax.random.key(seed)
    ks = iter(jax.random.split(key, 48))
    f32 = jnp.float32
    d = D_MODEL
    eh = EXPERT_HIDDEN

    def nrm(shape, s):
        return jax.random.normal(next(ks), shape, f32) * s

    x = nrm((BATCH, SEQ, d), 1.0)
    c = nrm((BATCH, d), 1.0)
    ctx = nrm((BATCH, CTX_LEN, d), 1.0)
    c_ctx = nrm((d,), 1.0)
    mod_w = nrm((DEPTH, d, 6 * d), 0.5 * d ** -0.5)
    mod_b = nrm((DEPTH, 6 * d), 0.02)
    ln_g = 1.0 + nrm((DEPTH, 2, d), 0.02)
    ln_b = nrm((DEPTH, 2, d), 0.02)
    router_w = nrm((DEPTH, d, N_EXPERTS), d ** -0.5)
    router_bias = nrm((DEPTH, N_EXPERTS), 0.01)
    exp_w_gu = nrm((DEPTH, N_EXPERTS, d, 2 * eh), d ** -0.5)
    exp_w_down = nrm((DEPTH, N_EXPERTS, eh, d), BETA * eh ** -0.5)
    sh_w_gu = nrm((DEPTH, d, 2 * eh), d ** -0.5)
    sh_w_down = nrm((DEPTH, eh, d), BETA * eh ** -0.5)
    ev_w_in = nrm((N_EVEN, d, EVEN_WIDTH), d ** -0.5)
    ssd_conv_w = nrm((N_EVEN, SSD_CONV, SSD_XBC), SSD_CONV ** -0.5)
    ssd_conv_b = nrm((N_EVEN, SSD_XBC), 0.02)
    ssd_a_log = jnp.log(jax.random.uniform(next(ks), (N_EVEN, 2, SSD_HEADS), f32, 1.0, 16.0))
    dt0 = jnp.exp(jax.random.uniform(next(ks), (N_EVEN, 2, SSD_HEADS), f32, math.log(1e-3), math.log(1e-1)))
    ssd_dt_bias = dt0 + jnp.log(-jnp.expm1(-dt0))
    ssd_d = 1.0 + nrm((N_EVEN, 2, SSD_HEADS), 0.1)
    ssd_norm = 1.0 + nrm((N_EVEN, SSD_INNER), 0.02)
    mla_q_norm = 1.0 + nrm((N_EVEN, MLA_Q_RANK), 0.02)
    mla_w_uq = nrm((N_EVEN, MLA_Q_RANK, MLA_HEADS * (MLA_NOPE + MLA_ROPE)), MLA_Q_RANK ** -0.5)
    mla_kv_norm = 1.0 + nrm((N_EVEN, MLA_KV_RANK), 0.02)
    mla_w_ukv = nrm((N_EVEN, MLA_KV_RANK, MLA_HEADS * (MLA_NOPE + MLA_V)), MLA_KV_RANK ** -0.5)
    ev_w_out = nrm((N_EVEN, EVEN_OUT, d), BETA * EVEN_OUT ** -0.5)
    od_w_in = nrm((N_ODD, d, ODD_WIDTH), d ** -0.5)
    f_bias = jnp.linspace(3.0, 6.0, MLSTM_HEADS, dtype=f32)
    i_bias = jnp.zeros((MLSTM_HEADS,), f32)
    mlstm_gate_b = nrm((N_ODD, 4, MLSTM_HEADS), 0.1) + jnp.stack([i_bias, f_bias, i_bias, f_bias])[None]
    mlstm_conv_w = nrm((N_ODD, MLSTM_CONV, 2 * MLSTM_HEADS * MLSTM_QK), MLSTM_CONV ** -0.5)
    mlstm_conv_b = nrm((N_ODD, 2 * MLSTM_HEADS * MLSTM_QK), 0.02)
    mlstm_norm = 1.0 + nrm((N_ODD, ODD_OUT), 0.02)
    od_w_out = nrm((N_ODD, ODD_OUT, d), BETA * ODD_OUT ** -0.5)
    return {'x': x, 'c': c, 'ctx': ctx, 'c_ctx': c_ctx, 'mod_w': mod_w, 'mod_b': mod_b,
            'ln_g': ln_g, 'ln_b': ln_b, 'router_w': router_w, 'router_bias': router_bias,
            'exp_w_gu': exp_w_gu, 'exp_w_down': exp_w_down, 'sh_w_gu': sh_w_gu, 'sh_w_down': sh_w_down,
            'ev_w_in': ev_w_in, 'ssd_conv_w': ssd_conv_w, 'ssd_conv_b': ssd_conv_b, 'ssd_a_log': ssd_a_log,
            'ssd_dt_bias': ssd_dt_bias, 'ssd_d': ssd_d, 'ssd_norm': ssd_norm, 'mla_q_norm': mla_q_norm,
            'mla_w_uq': mla_w_uq, 'mla_kv_norm': mla_kv_norm, 'mla_w_ukv': mla_w_ukv, 'ev_w_out': ev_w_out,
            'od_w_in': od_w_in, 'mlstm_gate_b': mlstm_gate_b, 'mlstm_conv_w': mlstm_conv_w,
            'mlstm_conv_b': mlstm_conv_b, 'mlstm_norm': mlstm_norm, 'od_w_out': od_w_out}


def reference(x, c, ctx, c_ctx, mod_w, mod_b, ln_g, ln_b, router_w, router_bias,
              exp_w_gu, exp_w_down, sh_w_gu, sh_w_down,
              ev_w_in, ssd_conv_w, ssd_conv_b, ssd_a_log, ssd_dt_bias, ssd_d, ssd_norm,
              mla_q_norm, mla_w_uq, mla_kv_norm, mla_w_ukv, ev_w_out,
              od_w_in, mlstm_gate_b, mlstm_conv_w, mlstm_conv_b, mlstm_norm, od_w_out):
    b, seq, d = x.shape
    ctx_len = ctx.shape[1]
    n_rows = seq // GRID_W
    cos, sin = axial_rope(n_rows)
    h_lat, h_ctx = x, ctx
    for i in range(DEPTH):
        last = i == DEPTH - 1
        j = i // 2
        m_lat = jax.nn.silu(c) @ mod_w[i] + mod_b[i]
        m_ctx = jax.nn.silu(c_ctx) @ mod_w[i] + mod_b[i]
        sh1, sc1, g1, sh2, sc2, g2 = jnp.split(m_lat[:, None, :], 6, axis=-1)
        csh1, csc1, cg1, csh2, csc2, cg2 = jnp.split(m_ctx, 6, axis=-1)
        u_lat = modulate(h_lat, sh1, sc1)
        u_ctx = modulate(h_ctx, csh1, csc1)
        if i % 2 == 0:
            o_lat, o_ctx = ssd_mla_mixer(u_lat, u_ctx, ev_w_in[j], ssd_conv_w[j], ssd_conv_b[j], ssd_a_log[j],
                                         ssd_dt_bias[j], ssd_d[j], ssd_norm[j], mla_q_norm[j], mla_w_uq[j],
                                         mla_kv_norm[j], mla_w_ukv[j], ev_w_out[j], cos, sin)
        else:
            o_lat, o_ctx = mlstm_mixer(u_lat, u_ctx, od_w_in[j], mlstm_gate_b[j], mlstm_conv_w[j],
                                       mlstm_conv_b[j], mlstm_norm[j], od_w_out[j])
        h_lat = layer_norm(ALPHA * h_lat + g1 * o_lat, ln_g[i, 0], ln_b[i, 0])
        u_lat = modulate(h_lat, sh2, sc2).reshape(b * seq, d)
        if last:
            f_lat = moe_ffn(u_lat, router_w[i], router_bias[i], exp_w_gu[i], exp_w_down[i],
                            sh_w_gu[i], sh_w_down[i]).reshape(b, seq, d)
        else:
            h_ctx = layer_norm(ALPHA * h_ctx + cg1 * o_ctx, ln_g[i, 0], ln_b[i, 0])
            u_ctx = modulate(h_ctx, csh2, csc2).reshape(b * ctx_len, d)
            f_all = moe_ffn(jnp.concatenate([u_lat, u_ctx], axis=0), router_w[i], router_bias[i],
                            exp_w_gu[i], exp_w_down[i], sh_w_gu[i], sh_w_down[i])
            f_lat = f_all[:b * seq].reshape(b, seq, d)
            f_ctx = f_all[b * seq:].reshape(b, ctx_len, d)
            h_ctx = layer_norm(ALPHA * h_ctx + cg2 * f_ctx, ln_g[i, 1], ln_b[i, 1])
        h_lat = layer_norm(ALPHA * h_lat + g2 * f_lat, ln_g[i, 1], ln_b[i, 1])
    return h_lat
```

```python
import functools
import math

import jax
import jax.numpy as jnp
from jax import lax
from jax.experimental import pallas as pl
from jax.experimental.pallas import tpu as pltpu

F32 = jnp.float32
BF16 = jnp.bfloat16

D_MODEL = 2048
BATCH = 4
SEQ = 4096
DEPTH = 4
GRID_W = 64
CTX_LEN = 256
N_LAT = BATCH * SEQ
N_CTX = BATCH * CTX_LEN
N_TOK = N_LAT + N_CTX

SSD_HEADS = 16
SSD_HEAD_DIM = 64
SSD_GROUPS = 2
SSD_STATE = 128
SSD_CONV = 4
SSD_CHUNK = 128
SSD_INNER = SSD_HEADS * SSD_HEAD_DIM
SSD_XBC = SSD_INNER + 2 * SSD_GROUPS * SSD_STATE

MLA_HEADS = 8
MLA_Q_RANK = 512
MLA_KV_RANK = 512
MLA_NOPE = 128
MLA_ROPE = 64
MLA_V = 128
MLA_SCALE = (MLA_NOPE + MLA_ROPE) ** -0.5
MLA_QPAD = 256
ROPE_BASE = 10000.0

MLSTM_HEADS = 8
MLSTM_QK = 128
MLSTM_V = 256
MLSTM_CONV = 4
MLSTM_CHUNK = 128

N_EXPERTS = 64
TOP_K = 8
EXPERT_HIDDEN = 384
ROUTE_SCALE = 2.5
MOE_ROWS = 256

ALPHA = (2 * DEPTH) ** 0.25
EPS = 1e-6
LN_EPS = 1e-5

EV_Z = 0
EV_XBC = SSD_INNER
EV_CQ = EV_XBC + SSD_XBC
EV_CKV = EV_CQ + MLA_Q_RANK
EV_KR = EV_CKV + MLA_KV_RANK
EV_DTF = EV_KR + MLA_ROPE
EV_DTB = EV_DTF + SSD_HEADS
EV_WIDTH = 3840
OD_QK = 0
OD_V = 2 * MLSTM_HEADS * MLSTM_QK
OD_O = OD_V + MLSTM_HEADS * MLSTM_V
OD_G = OD_O + MLSTM_HEADS * MLSTM_V
OD_WIDTH = 6272
OD_OUT = MLSTM_HEADS * MLSTM_V

TM = 512
VMEM_LIMIT = 56 * 1024 * 1024


def _cparams(sem):
    return pltpu.CompilerParams(dimension_semantics=sem, vmem_limit_bytes=VMEM_LIMIT)


def _mod_group(i, tm):
    return jnp.minimum(i * tm // SEQ, BATCH)


def _mod_spec(which, tm):
    return pl.BlockSpec((1, 1, D_MODEL), lambda i, *_: (_mod_group(i, tm) * 6 + which, 0, 0))


def _silu(v):
    return v * jax.nn.sigmoid(v)


def _mod_body(c_ref, w_ref, b_ref, o_ref):
    a = _silu(c_ref[...]).astype(BF16)
    o_ref[0] = jnp.dot(a, w_ref[0].astype(BF16), preferred_element_type=F32) + b_ref[0]


def mod_table(c, c_ctx, mod_w, mod_b):
    c8 = jnp.zeros((8, D_MODEL), F32).at[:BATCH].set(c).at[BATCH].set(c_ctx)
    tn = 1024
    n = 6 * D_MODEL
    out = pl.pallas_call(
        _mod_body,
        out_shape=jax.ShapeDtypeStruct((DEPTH, 8, n), F32),
        grid=(DEPTH, n // tn),
        in_specs=[pl.BlockSpec((8, D_MODEL), lambda l, j: (0, 0)),
                  pl.BlockSpec((1, D_MODEL, tn), lambda l, j: (l, 0, j)),
                  pl.BlockSpec((1, 1, tn), lambda l, j: (l, 0, j))],
        out_specs=pl.BlockSpec((1, 8, tn), lambda l, j: (l, 0, j)),
        compiler_params=_cparams(("parallel", "parallel")),
        name="mod_table",
    )(c8, mod_w, mod_b.reshape(DEPTH, 1, n))
    return out[:, :BATCH + 1].reshape(DEPTH, (BATCH + 1) * 6, 1, D_MODEL)


def fused_matmul(a_ins, pro_fn, w, *, k, tn, out_dtype, name, epi_ins=(), epi_fn=None, m=N_TOK, tm=TM):
    n = w.shape[1]
    na, ne = len(a_ins), len(epi_ins)

    def body(*refs):
        a_refs = refs[:na]
        w_ref = refs[na]
        e_refs = refs[na + 1:na + 1 + ne]
        o_ref = refs[na + 1 + ne]
        a_scr = refs[na + 2 + ne]

        @pl.when(pl.program_id(1) == 0)
        def _():
            pro_fn(a_refs, a_scr)

        acc = jnp.dot(a_scr[...], w_ref[...], preferred_element_type=F32)
        if epi_fn is not None:
            acc = epi_fn(acc, e_refs)
        o_ref[...] = acc.astype(out_dtype)

    in_specs = ([spec for _, spec in a_ins]
                + [pl.BlockSpec((k, tn), lambda i, j: (0, j))]
                + [spec for _, spec in epi_ins])
    return pl.pallas_call(
        body,
        out_shape=jax.ShapeDtypeStruct((m, n), out_dtype),
        grid=(m // tm, n // tn),
        in_specs=in_specs,
        out_specs=pl.BlockSpec((tm, tn), lambda i, j: (i, j)),
        scratch_shapes=[pltpu.VMEM((tm, k), BF16)],
        compiler_params=_cparams(("parallel", "arbitrary")),
        name=name,
    )(*[a for a, _ in a_ins], w, *[a for a, _ in epi_ins])


def _pro_modulate(a_refs, a_scr):
    h_ref, sh_ref, sc_ref = a_refs
    a_scr[...] = (h_ref[...] * (1.0 + sc_ref[0]) + sh_ref[0]).astype(BF16)


def in_proj(h, mods, w, tn, name):
    a_ins = [(h, pl.BlockSpec((TM, D_MODEL), lambda i, j: (i, 0))),
             (mods, _mod_spec(0, TM)), (mods, _mod_spec(1, TM))]
    return fused_matmul(a_ins, _pro_modulate, w, k=D_MODEL, tn=tn, out_dtype=F32, name=name)


def _pro_rms(a_refs, a_scr):
    x_ref, g_ref = a_refs
    x = x_ref[...]
    y = x * lax.rsqrt(jnp.mean(x * x, axis=-1, keepdims=True) + EPS) * g_ref[...]
    a_scr[...] = y.astype(BF16)


def _rope_rotate(t, cos, sin):
    lane = lax.broadcasted_iota(jnp.int32, t.shape, 1)
    n = t.shape[1]
    partner = jnp.where((lane // 16) % 2 == 0, pltpu.roll(t, n - 16, 1), pltpu.roll(t, 16, 1))
    return t * cos + partner * sin


def _epi_q(acc, e_refs):
    cos_ref, sin_ref = e_refs
    return _rope_rotate(acc * MLA_SCALE, cos_ref[...], sin_ref[...])


def _rope_block(i):
    return jnp.where(i < N_LAT // TM, i % (SEQ // TM), SEQ // TM)


def q_up(p, q_norm, w_uq, cos_q, sin_q):
    a_ins = [(p, pl.BlockSpec((TM, MLA_Q_RANK), lambda i, j: (i, EV_CQ // MLA_Q_RANK))),
             (q_norm, pl.BlockSpec((1, MLA_Q_RANK), lambda i, j: (0, 0)))]
    tn = 512
    epi_ins = [(cos_q, pl.BlockSpec((TM, tn), lambda i, j: (_rope_block(i), 0))),
               (sin_q, pl.BlockSpec((TM, tn), lambda i, j: (_rope_block(i), 0)))]
    return fused_matmul(a_ins, _pro_rms, w_uq, k=MLA_Q_RANK, tn=tn, out_dtype=BF16, name="q_up",
                        epi_ins=epi_ins, epi_fn=_epi_q)


def kv_up(p, kv_norm, w_ukv):
    a_ins = [(p, pl.BlockSpec((TM, MLA_KV_RANK), lambda i, j: (i, EV_CKV // MLA_KV_RANK))),
             (kv_norm, pl.BlockSpec((1, MLA_KV_RANK), lambda i, j: (0, 0)))]
    return fused_matmul(a_ins, _pro_rms, w_ukv, k=MLA_KV_RANK, tn=1024, out_dtype=BF16, name="kv_up")


def _krope_body(p_ref, cos_ref, sin_ref, o_ref):
    t = p_ref[...]
    lane = lax.broadcasted_iota(jnp.int32, t.shape, 1)
    t = jnp.where(lane < MLA_ROPE, t, 0.0)
    o_ref[...] = _rope_rotate(t, cos_ref[...], sin_ref[...]).astype(BF16)


def k_rope(p, cos_k, sin_k):
    return pl.pallas_call(
        _krope_body,
        out_shape=jax.ShapeDtypeStruct((N_TOK, 128), BF16),
        grid=(N_TOK // TM,),
        in_specs=[pl.BlockSpec((TM, 128), lambda i: (i, EV_KR // 128)),
                  pl.BlockSpec((TM, 128), lambda i: (_rope_block(i), 0)),
                  pl.BlockSpec((TM, 128), lambda i: (_rope_block(i), 0))],
        out_specs=pl.BlockSpec((TM, 128), lambda i: (i, 0)),
        compiler_params=_cparams(("parallel",)),
        name="k_rope",
    )(p, cos_k, sin_k)


def _attn_body(*refs, n_seg):
    q_ref = refs[0]
    kn_refs = refs[1:1 + n_seg]
    kr_refs = refs[1 + n_seg:1 + 2 * n_seg]
    v_refs = refs[1 + 2 * n_seg:1 + 3 * n_seg]
    o_ref = refs[1 + 3 * n_seg]
    k_scr, v_scr = refs[2 + 3 * n_seg:]

    @pl.when(pl.program_id(2) == 0)
    def _():
        off = 0
        for kn, kr, v in zip(kn_refs, kr_refs, v_refs):
            rows = kn.shape[0]
            k_scr[off:off + rows, 0:MLA_NOPE] = kn[...]
            k_scr[off:off + rows, MLA_NOPE:MLA_QPAD] = kr[...]
            v_scr[off:off + rows, :] = v[...]
            off += rows

    s = lax.dot_general(q_ref[...], k_scr[...], (((1,), (1,)), ((), ())), preferred_element_type=F32)
    m = jnp.max(s, axis=-1, keepdims=True)
    p = jnp.exp(s - m)
    l = jnp.sum(p, axis=-1, keepdims=True)
    o = jnp.dot(p.astype(BF16), v_scr[...], preferred_element_type=F32)
    o_ref[...] = (o / l).astype(o_ref.dtype)


def attention(q, kv, kr, *, latent):
    tq = 256
    ctx_blk = N_LAT // CTX_LEN
    if latent:
        nq = SEQ // tq
        q_map = lambda b, h, i: (b * nq + i, h)
        segs = [(SEQ, lambda b: b), (CTX_LEN, lambda b: ctx_blk + b)]
        rows = N_LAT
    else:
        nq = CTX_LEN // tq
        q_map = lambda b, h, i: (N_LAT // tq + b * nq + i, h)
        segs = [(CTX_LEN, lambda b: ctx_blk + b)]
        rows = N_CTX
    lk = sum(r for r, _ in segs)
    in_specs = [pl.BlockSpec((tq, MLA_QPAD), q_map)]
    in_specs += [pl.BlockSpec((r, MLA_NOPE), lambda b, h, i, f=f: (f(b), 2 * h)) for r, f in segs]
    in_specs += [pl.BlockSpec((r, 128), lambda b, h, i, f=f: (f(b), 0)) for r, f in segs]
    in_specs += [pl.BlockSpec((r, MLA_V), lambda b, h, i, f=f: (f(b), 2 * h + 1)) for r, f in segs]
    n_seg = len(segs)
    out_map = (lambda b, h, i: (b * nq + i, h))
    return pl.pallas_call(
        functools.partial(_attn_body, n_seg=n_seg),
        out_shape=jax.ShapeDtypeStruct((rows, MLA_HEADS * MLA_V), BF16),
        grid=(BATCH, MLA_HEADS, nq),
        in_specs=in_specs,
        out_specs=pl.BlockSpec((tq, MLA_V), out_map),
        scratch_shapes=[pltpu.VMEM((lk, MLA_QPAD), BF16), pltpu.VMEM((lk, MLA_V), BF16)],
        compiler_params=_cparams(("parallel", "parallel", "arbitrary")),
        name="attn_lat" if latent else "attn_ctx",
    )(q, *([kv] * n_seg), *([kr] * n_seg), *([kv] * n_seg))


def _pro_even_out(a_refs, a_scr):
    y_ref, z_ref, g_ref, o_ref = a_refs
    t = y_ref[...] * _silu(z_ref[...])
    t = t * lax.rsqrt(jnp.mean(t * t, axis=-1, keepdims=True) + EPS) * g_ref[...]
    a_scr[:, 0:SSD_INNER] = t.astype(BF16)
    a_scr[:, SSD_INNER:] = o_ref[...]


def even_out_proj(y, p, ssd_g, o_attn, w_out):
    a_ins = [(y, pl.BlockSpec((TM, SSD_INNER), lambda i, j: (i, 0))),
             (p, pl.BlockSpec((TM, SSD_INNER), lambda i, j: (i, EV_Z // SSD_INNER))),
             (ssd_g, pl.BlockSpec((1, SSD_INNER), lambda i, j: (0, 0))),
             (o_attn, pl.BlockSpec((TM, MLA_HEADS * MLA_V), lambda i, j: (i, 0)))]
    return fused_matmul(a_ins, _pro_even_out, w_out, k=2 * SSD_INNER, tn=1024, out_dtype=F32,
                        name="even_out_proj")


def _pro_odd_out(a_refs, a_scr):
    h_ref, o_ref, g_ref = a_refs
    for hd in range(MLSTM_HEADS):
        sl = slice(hd * MLSTM_V, (hd + 1) * MLSTM_V)
        hh = h_ref[:, sl]
        mu = jnp.mean(hh, axis=-1, keepdims=True)
        d = hh - mu
        var = jnp.mean(d * d, axis=-1, keepdims=True)
        hn = d * lax.rsqrt(var + EPS) * g_ref[:, sl]
        a_scr[:, sl] = (hn * jax.nn.sigmoid(o_ref[:, sl])).astype(BF16)


def odd_out_proj(hh, p, norm_g, w_out):
    a_ins = [(hh, pl.BlockSpec((TM, OD_OUT), lambda i, j: (i, 0))),
             (p, pl.BlockSpec((TM, OD_OUT), lambda i, j: (i, OD_O // OD_OUT))),
             (norm_g, pl.BlockSpec((1, OD_OUT), lambda i, j: (0, 0)))]
    return fused_matmul(a_ins, _pro_odd_out, w_out, k=OD_OUT, tn=1024, out_dtype=F32, name="odd_out_proj")


def _ln_body(h_ref, o_ref, g_ref, lg_ref, lb_ref, *rest, modulate):
    x = ALPHA * h_ref[...] + g_ref[0] * o_ref[...]
    mu = jnp.mean(x, axis=-1, keepdims=True)
    d = x - mu
    var = jnp.mean(d * d, axis=-1, keepdims=True)
    y = d * lax.rsqrt(var + LN_EPS) * lg_ref[...] + lb_ref[...]
    if modulate:
        sh_ref, sc_ref, hn_ref, u_ref = rest
        hn_ref[...] = y
        u_ref[...] = y * (1.0 + sc_ref[0]) + sh_ref[0]
    else:
        (hn_ref,) = rest
        hn_ref[...] = y


def residual_ln(h, o, mods, gate_idx, ln_g, ln_b, *, modulate, m=N_TOK):
    tm = 256
    row = pl.BlockSpec((tm, D_MODEL), lambda i: (i, 0))
    vec = pl.BlockSpec((1, D_MODEL), lambda i: (0, 0))
    in_specs = [row, row, _mod_spec(gate_idx, tm), vec, vec]
    args = [h, o, mods, ln_g.reshape(1, D_MODEL), ln_b.reshape(1, D_MODEL)]
    out_shape = [jax.ShapeDtypeStruct((m, D_MODEL), F32)]
    out_specs = [row]
    if modulate:
        in_specs += [_mod_spec(3, tm), _mod_spec(4, tm)]
        args += [mods, mods]
        out_shape.append(jax.ShapeDtypeStruct((m, D_MODEL), F32))
        out_specs.append(row)
    return pl.pallas_call(
        functools.partial(_ln_body, modulate=modulate),
        out_shape=out_shape,
        grid=(m // tm,),
        in_specs=in_specs,
        out_specs=out_specs,
        compiler_params=_cparams(("parallel",)),
        name="residual_ln",
    )(*args)


def _router_body(u_ref, w_ref, o_ref):
    o_ref[...] = jnp.dot(u_ref[...], w_ref[...], preferred_element_type=F32, precision=lax.Precision.HIGHEST)


def router_logits(u, router_w):
    m = u.shape[0]
    tm = 512
    w = jnp.zeros((D_MODEL, 128), F32).at[:, :N_EXPERTS].set(router_w)
    out = pl.pallas_call(
        _router_body,
        out_shape=jax.ShapeDtypeStruct((m, 128), F32),
        grid=(m // tm,),
        in_specs=[pl.BlockSpec((tm, D_MODEL), lambda i: (i, 0)),
                  pl.BlockSpec((D_MODEL, 128), lambda i: (0, 0))],
        out_specs=pl.BlockSpec((tm, 128), lambda i: (i, 0)),
        compiler_params=_cparams(("parallel",)),
        name="router",
    )(u, w)
    return out[:, :N_EXPERTS]


def _swiglu(x_bf, wgu, wd):
    hgu = jnp.dot(x_bf, wgu, preferred_element_type=F32)
    act = _silu(hgu[:, :EXPERT_HIDDEN]) * hgu[:, EXPERT_HIDDEN:]
    return jnp.dot(act.astype(BF16), wd, preferred_element_type=F32)


def _experts_body(be_ref, nu_ref, x_ref, wgu_ref, wd_ref, rw_ref, o_ref, wgu_scr, wd_scr):
    i = pl.program_id(0)
    changed = jnp.logical_or(i == 0, be_ref[i] != be_ref[jnp.maximum(i - 1, 0)])

    @pl.when(changed)
    def _():
        wgu_scr[...] = wgu_ref[0].astype(BF16)
        wd_scr[...] = wd_ref[0].astype(BF16)

    @pl.when(i < nu_ref[0])
    def _():
        o_ref[...] = _swiglu(x_ref[...], wgu_scr[...], wd_scr[...]) * rw_ref[...]

    @pl.when(i >= nu_ref[0])
    def _():
        o_ref[...] = jnp.zeros_like(o_ref)


def experts(xs, blk_e, n_used, row_w, w_gu, w_down):
    n_rows = xs.shape[0]
    nb = n_rows // MOE_ROWS
    grid_spec = pltpu.PrefetchScalarGridSpec(
        num_scalar_prefetch=2,
        grid=(nb,),
        in_specs=[pl.BlockSpec((MOE_ROWS, D_MODEL), lambda i, be, nu: (i, 0)),
                  pl.BlockSpec((1, D_MODEL, 2 * EXPERT_HIDDEN), lambda i, be, nu: (be[i], 0, 0)),
                  pl.BlockSpec((1, EXPERT_HIDDEN, D_MODEL), lambda i, be, nu: (be[i], 0, 0)),
                  pl.BlockSpec((MOE_ROWS, 1), lambda i, be, nu: (i, 0))],
        out_specs=pl.BlockSpec((MOE_ROWS, D_MODEL), lambda i, be, nu: (i, 0)),
        scratch_shapes=[pltpu.VMEM((D_MODEL, 2 * EXPERT_HIDDEN), BF16),
                        pltpu.VMEM((EXPERT_HIDDEN, D_MODEL), BF16)],
    )
    return pl.pallas_call(
        _experts_body,
        out_shape=jax.ShapeDtypeStruct((n_rows, D_MODEL), F32),
        grid_spec=grid_spec,
        compiler_params=_cparams(("arbitrary",)),
        name="experts",
    )(blk_e, n_used, xs, w_gu, w_down, row_w)


def _shared_body(x_ref, wgu_ref, wd_ref, o_ref):
    o_ref[...] = _swiglu(x_ref[...].astype(BF16), wgu_ref[...], wd_ref[...])


def shared_expert(u, sh_gu, sh_down):
    m = u.shape[0]
    tm = 512
    return pl.pallas_call(
        _shared_body,
        out_shape=jax.ShapeDtypeStruct((m, D_MODEL), F32),
        grid=(m // tm,),
        in_specs=[pl.BlockSpec((tm, D_MODEL), lambda i: (i, 0)),
                  pl.BlockSpec((D_MODEL, 2 * EXPERT_HIDDEN), lambda i: (0, 0)),
                  pl.BlockSpec((EXPERT_HIDDEN, D_MODEL), lambda i: (0, 0))],
        out_specs=pl.BlockSpec((tm, D_MODEL), lambda i: (i, 0)),
        compiler_params=_cparams(("parallel",)),
        name="shared_expert",
    )(u, sh_gu.astype(BF16), sh_down.astype(BF16))


def moe_ffn(u, router_w, router_bias, w_gu, w_down, sh_gu, sh_down):
    t = u.shape[0]
    scores = jax.nn.sigmoid(router_logits(u, router_w))
    _, idx = lax.top_k(scores + router_bias, TOP_K)
    gates = jnp.take_along_axis(scores, idx, axis=-1)
    gates = ROUTE_SCALE * gates / jnp.sum(gates, axis=-1, keepdims=True)
    sel = jnp.zeros((t, N_EXPERTS), jnp.int32).at[jnp.arange(t)[:, None], idx].set(1)
    incl = jnp.cumsum(sel, axis=0)
    counts = incl[-1]
    rank = jnp.take_along_axis(incl - sel, idx, axis=-1)
    padded = (counts + MOE_ROWS - 1) // MOE_ROWS * MOE_ROWS
    pends = jnp.cumsum(padded)
    dest = (pends - padded)[idx] + rank
    n_blocks = (t * TOP_K + N_EXPERTS * (MOE_ROWS - 1) + MOE_ROWS - 1) // MOE_ROWS
    n_rows = n_blocks * MOE_ROWS
    flat_dest = dest.reshape(-1)
    tok = jnp.arange(t * TOP_K, dtype=jnp.int32) // TOP_K
    row_tok = jnp.full((n_rows,), t, jnp.int32).at[flat_dest].set(tok)
    row_w = jnp.zeros((n_rows,), F32).at[flat_dest].set(gates.reshape(-1))
    blk_e = jnp.minimum(jnp.searchsorted(pends, jnp.arange(n_blocks) * MOE_ROWS, side='right'),
                        N_EXPERTS - 1).astype(jnp.int32)
    n_used = (pends[-1] // MOE_ROWS).astype(jnp.int32).reshape(1)
    u_pad = jnp.concatenate([u.astype(BF16), jnp.zeros((1, D_MODEL), BF16)], axis=0)
    xs = u_pad[row_tok]
    y = experts(xs, blk_e, n_used, row_w.reshape(n_rows, 1), w_gu, w_down)
    routed = jnp.sum(y[flat_dest].reshape(t, TOP_K, D_MODEL), axis=1)
    return routed + shared_expert(u, sh_gu, sh_down)


def _split_stream(a):
    c = a.shape[-1]
    return a[:N_LAT].reshape(BATCH, SEQ, c), a[N_LAT:].reshape(BATCH, CTX_LEN, c)


def _join_stream(lat, ctx):
    c = lat.shape[-1]
    return jnp.concatenate([lat.reshape(N_LAT, c), ctx.reshape(N_CTX, c)], axis=0)


def _dwconv(h, w, bias):
    k = w.shape[0]
    y = lax.conv_general_dilated(h, w[:, None, :].astype(h.dtype), window_strides=(1,),
                                 padding=[((k - 1) // 2, k // 2)],
                                 dimension_numbers=('NWC', 'WIO', 'NWC'),
                                 feature_group_count=h.shape[-1])
    return y + bias


def _segsum(a):
    t = a.shape[-1]
    a_rep = jnp.broadcast_to(a[..., :, None], a.shape + (t,))
    strict = jnp.tril(jnp.ones((t, t), bool), -1)
    cs = jnp.cumsum(jnp.where(strict, a_rep, 0.0), axis=-2)
    return jnp.where(jnp.tril(jnp.ones((t, t), bool)), cs, -jnp.inf)


def _ssd_chunked(x, a, bm, cm, h0):
    b, L, G, E, P = x.shape
    q = SSD_CHUNK
    nc = L // q
    x = x.reshape(b, nc, q, G, E, P)
    bm = bm.reshape(b, nc, q, G, -1)
    cm = cm.reshape(b, nc, q, G, -1)
    a = jnp.moveaxis(a.reshape(b, nc, q, G, E), (3, 4), (1, 2))
    a_cum = jnp.cumsum(a, axis=-1)
    lmat = jnp.exp(_segsum(a))
    cb = jnp.einsum('bclgn,bcsgn->bgcls', cm, bm)
    y_diag = jnp.einsum('bgecls,bcsgep->bclgep', cb[:, :, None] * lmat, x)
    decay_in = jnp.moveaxis(jnp.exp(a_cum[..., -1:] - a_cum), (1, 2), (3, 4))
    states = jnp.einsum('bclgn,bclgep->bcgepn', bm, x * decay_in[..., None])
    states = jnp.concatenate([h0[:, None], states], axis=1)
    chunk_a = jnp.pad(a_cum[..., -1], ((0, 0), (0, 0), (0, 0), (1, 0)))
    chunk_decay = jnp.exp(_segsum(chunk_a))
    states = jnp.einsum('bgezc,bcgepn->bzgepn', chunk_decay, states)
    y_off = jnp.einsum('bclgn,bcgepn,bgecl->bclgep', cm, states[:, :-1], jnp.exp(a_cum))
    y = (y_diag + y_off).reshape(b, L, G, E, P)
    return y, states[:, -1]


def _ssd_direction(xs, dt_raw, bm, cm, a_log, dt_bias, d_skip, h0):
    b, L, _ = xs.shape
    e = SSD_HEADS // SSD_GROUPS
    dt = jax.nn.softplus(dt_raw + dt_bias).reshape(b, L, SSD_GROUPS, e)
    xh = xs.reshape(b, L, SSD_GROUPS, e, SSD_HEAD_DIM)
    a_cont = -jnp.exp(a_log).reshape(SSD_GROUPS, e)
    y, h_fin = _ssd_chunked(xh * dt[..., None], dt * a_cont, bm, cm, h0)
    y = y + d_skip.reshape(SSD_GROUPS, e)[..., None] * xh
    return y.reshape(b, L, SSD_INNER), h_fin


def _ssd_bidir(xs, dt_f, dt_b, bm, cm, a_log, dt_bias, d_skip, h0_f, h0_b):
    y_f, h_f = _ssd_direction(xs, dt_f, bm, cm, a_log[0], dt_bias[0], d_skip[0], h0_f)
    y_b, h_b = _ssd_direction(xs[:, ::-1], dt_b[:, ::-1], bm[:, ::-1], cm[:, ::-1],
                              a_log[1], dt_bias[1], d_skip[1], h0_b)
    return y_f + y_b[:, ::-1], h_f, h_b


def ssd_mixer(p, conv_w, conv_b, a_log, dt_bias, d_skip):
    def prep(pp):
        xbc = _silu(_dwconv(pp[..., EV_XBC:EV_XBC + SSD_XBC], conv_w, conv_b))
        xs = xbc[..., :SSD_INNER]
        gs = SSD_GROUPS * SSD_STATE
        bm = xbc[..., SSD_INNER:SSD_INNER + gs].reshape(pp.shape[:2] + (SSD_GROUPS, SSD_STATE))
        cm = xbc[..., SSD_INNER + gs:].reshape(pp.shape[:2] + (SSD_GROUPS, SSD_STATE))
        return xs, pp[..., EV_DTF:EV_DTF + SSD_HEADS], pp[..., EV_DTB:EV_DTB + SSD_HEADS], bm, cm

    p_l, p_c = _split_stream(p)
    h_zero = jnp.zeros((BATCH, SSD_GROUPS, SSD_HEADS // SSD_GROUPS, SSD_HEAD_DIM, SSD_STATE), F32)
    y_c, hs_f, hs_b = _ssd_bidir(*prep(p_c), a_log, dt_bias, d_skip, h_zero, h_zero)
    y_l, _, _ = _ssd_bidir(*prep(p_l), a_log, dt_bias, d_skip, hs_f, hs_b)
    return _join_stream(y_l, y_c)


def _mlstm_chunked(q, k, v, ig, lf, state0):
    b, h, L, dk = q.shape
    dv = v.shape[-1]
    qc = MLSTM_CHUNK
    nc = L // qc
    q = q.reshape(b, h, nc, qc, dk)
    k = k.reshape(b, h, nc, qc, dk)
    v = v.reshape(b, h, nc, qc, dv)
    ig = ig.reshape(b, h, nc, qc)
    bcum = jnp.cumsum(lf.reshape(b, h, nc, qc), axis=-1)
    causal = jnp.tril(jnp.ones((qc, qc), bool))
    log_w = jnp.where(causal, bcum[..., :, None] - bcum[..., None, :] + ig[..., None, :], -jnp.inf)
    g = bcum[..., -1:] - bcum + ig
    g_max = jnp.max(g, axis=-1)
    wk = jnp.exp(g - g_max[..., None])[..., None] * k
    c_loc = jnp.einsum('bhcsv,bhcsk->bhcvk', v, wk)
    n_loc = jnp.sum(wk, axis=-2)

    def step(carry, inp):
        c_st, n_st, m_st = carry
        cl, nl, gm, bl = inp
        m_new = jnp.maximum(bl + m_st, gm)
        dec = jnp.exp(bl + m_st - m_new)
        sc = jnp.exp(gm - m_new)
        c_new = dec[..., None, None] * c_st + sc[..., None, None] * cl
        n_new = dec[..., None] * n_st + sc[..., None] * nl
        return (c_new, n_new, m_new), (c_st, n_st, m_st)

    xs = (jnp.moveaxis(c_loc, 2, 0), jnp.moveaxis(n_loc, 2, 0), jnp.moveaxis(g_max, 2, 0),
          jnp.moveaxis(bcum[..., -1], 2, 0))
    final, (c_in, n_in, m_in) = lax.scan(step, state0, xs)
    c_in = jnp.moveaxis(c_in, 0, 2)
    n_in = jnp.moveaxis(n_in, 0, 2)
    m_in = jnp.moveaxis(m_in, 0, 2)
    m_inter = bcum + m_in[..., None]
    m_t = jnp.maximum(m_inter, jnp.max(log_w, axis=-1))
    s = jnp.einsum('bhctk,bhcsk->bhcts', q, k) * jnp.exp(log_w - m_t[..., None])
    w_inter = jnp.exp(m_inter - m_t)
    num = jnp.einsum('bhcts,bhcsv->bhctv', s, v) + w_inter[..., None] * jnp.einsum('bhctk,bhcvk->bhctv', q, c_in)
    den = jnp.sum(s, axis=-1) + w_inter * jnp.einsum('bhctk,bhck->bhct', q, n_in)
    out = num / jnp.maximum(jnp.abs(den), jnp.exp(-m_t))[..., None]
    return out.reshape(b, h, L, dv), final


def _mlstm_bidir(q, k, v, ig_f, lf_f, ig_b, lf_b, state_f, state_b):
    h_f, s_f = _mlstm_chunked(q, k, v, ig_f, lf_f, state_f)
    r = lambda t: t[:, :, ::-1]
    h_b, s_b = _mlstm_chunked(r(q), r(k), r(v), r(ig_b), r(lf_b), state_b)
    return h_f + r(h_b), s_f, s_b


def mlstm_mixer(p, gate_b, conv_w, conv_b):
    def prep(pp):
        b, L, _ = pp.shape
        qk = _silu(_dwconv(pp[..., OD_QK:OD_V], conv_w, conv_b))
        q, k = jnp.split(qk, 2, axis=-1)
        heads = lambda t, d: jnp.moveaxis(t.reshape(b, L, MLSTM_HEADS, d), 2, 1)
        q = heads(q, MLSTM_QK)
        k = heads(k, MLSTM_QK) * MLSTM_QK ** -0.5
        v = heads(pp[..., OD_V:OD_O], MLSTM_V)
        g = pp[..., OD_G:OD_G + 4 * MLSTM_HEADS].reshape(b, L, 4, MLSTM_HEADS) + gate_b
        g = jnp.moveaxis(g, (2, 3), (0, 2))
        return q, k, v, g[0], jax.nn.log_sigmoid(g[1]), g[2], jax.nn.log_sigmoid(g[3])

    p_l, p_c = _split_stream(p)
    zero = (jnp.zeros((BATCH, MLSTM_HEADS, MLSTM_V, MLSTM_QK), F32),
            jnp.zeros((BATCH, MLSTM_HEADS, MLSTM_QK), F32),
            jnp.zeros((BATCH, MLSTM_HEADS), F32))
    h_c, s_f, s_b = _mlstm_bidir(*prep(p_c), zero, zero)
    h_l, _, _ = _mlstm_bidir(*prep(p_l), s_f, s_b)
    flat = lambda t: jnp.moveaxis(t, 1, 2).reshape(t.shape[0], t.shape[2], OD_OUT)
    return _join_stream(flat(h_l), flat(h_c))


def _rope_tables():
    n_rows = SEQ // GRID_W
    n_freq = MLA_ROPE // 4
    inv = ROPE_BASE ** (-jnp.arange(n_freq, dtype=F32) / n_freq)
    pos = jnp.arange(SEQ)
    ang_r = (pos // GRID_W).astype(F32)[:, None] * inv
    ang_c = (pos % GRID_W).astype(F32)[:, None] * inv
    cos = jnp.concatenate([jnp.cos(ang_r)] * 2 + [jnp.cos(ang_c)] * 2, axis=-1)
    sin = jnp.concatenate([-jnp.sin(ang_r), jnp.sin(ang_r), -jnp.sin(ang_c), jnp.sin(ang_c)], axis=-1)

    def widen(tab, fill, left, total):
        full = jnp.full((SEQ + TM, total), fill, F32)
        return full.at[:SEQ, left:left + MLA_ROPE].set(tab)

    cos_k, sin_k = widen(cos, 1.0, 0, 128), widen(sin, 0.0, 0, 128)
    cos_h, sin_h = widen(cos, 1.0, MLA_NOPE, MLA_QPAD), widen(sin, 0.0, MLA_NOPE, MLA_QPAD)
    cos_q = jnp.tile(cos_h, (1, 2))
    sin_q = jnp.tile(sin_h, (1, 2))
    return cos_q, sin_q, cos_k, sin_k


def _even_w_in(w):
    z, xbc, dt_f, dt_b, c_q, c_kv, k_r = jnp.split(
        w, [1024, 2560, 2576, 2592, 3104, 3616], axis=-1)
    pad = jnp.zeros((D_MODEL, EV_WIDTH - w.shape[1]), w.dtype)
    return jnp.concatenate([z, xbc, c_q, c_kv, k_r, dt_f, dt_b, pad], axis=-1).astype(BF16)


def _uq_padded(w_uq):
    w = w_uq.reshape(MLA_Q_RANK, MLA_HEADS, MLA_NOPE + MLA_ROPE)
    w = jnp.pad(w, ((0, 0), (0, 0), (0, MLA_QPAD - MLA_NOPE - MLA_ROPE)))
    return w.reshape(MLA_Q_RANK, MLA_HEADS * MLA_QPAD).astype(BF16)


def kernel(x, c, ctx, c_ctx, mod_w, mod_b, ln_g, ln_b, router_w, router_bias, exp_w_gu, exp_w_down,
           sh_w_gu, sh_w_down, ev_w_in, ssd_conv_w, ssd_conv_b, ssd_a_log, ssd_dt_bias, ssd_d, ssd_norm,
           mla_q_norm, mla_w_uq, mla_kv_norm, mla_w_ukv, ev_w_out, od_w_in, mlstm_gate_b, mlstm_conv_w,
           mlstm_conv_b, mlstm_norm, od_w_out):
    h = jnp.concatenate([x.reshape(N_LAT, D_MODEL), ctx.reshape(N_CTX, D_MODEL)], axis=0)
    mods_all = mod_table(c, c_ctx, mod_w, mod_b)
    cos_q, sin_q, cos_k, sin_k = _rope_tables()
    for i in range(DEPTH):
        last = i == DEPTH - 1
        j = i // 2
        mods = mods_all[i]
        if i % 2 == 0:
            p = in_proj(h, mods, _even_w_in(ev_w_in[j]), 768, "even_in_proj")
            y = ssd_mixer(p, ssd_conv_w[j], ssd_conv_b[j], ssd_a_log[j], ssd_dt_bias[j], ssd_d[j])
            q = q_up(p, mla_q_norm[j].reshape(1, -1), _uq_padded(mla_w_uq[j]), cos_q, sin_q)
            kv = kv_up(p, mla_kv_norm[j].reshape(1, -1), mla_w_ukv[j].astype(BF16))
            kr = k_rope(p, cos_k, sin_k)
            o_attn = jnp.concatenate([attention(q, kv, kr, latent=True),
                                      attention(q, kv, kr, latent=False)], axis=0)
            o = even_out_proj(y, p, ssd_norm[j].reshape(1, -1), o_attn, ev_w_out[j].astype(BF16))
        else:
            w_in = jnp.pad(od_w_in[j], ((0, 0), (0, OD_WIDTH - od_w_in.shape[-1]))).astype(BF16)
            p = in_proj(h, mods, w_in, 896, "odd_in_proj")
            hh = mlstm_mixer(p, mlstm_gate_b[j], mlstm_conv_w[j], mlstm_conv_b[j])
            o = odd_out_proj(hh, p, mlstm_norm[j].reshape(1, -1), od_w_out[j].astype(BF16))
        h, u = residual_ln(h, o, mods, 2, ln_g[i, 0], ln_b[i, 0], modulate=True)
        m = N_LAT if last else N_TOK
        f = moe_ffn(u[:m], router_w[i], router_bias[i], exp_w_gu[i], exp_w_down[i], sh_w_gu[i], sh_w_down[i])
        (h,) = residual_ln(h[:m], f, mods, 5, ln_g[i, 1], ln_b[i, 1], modulate=False, m=m)
    return h[:N_LAT].reshape(BATCH, SEQ, D_MODEL)
```
